```python
import jax, jax.numpy as jnp
from jax import lax
import numpy as np

D_MODEL = 1024
BATCH = 8
SEQ = 2048
DEPTH = 1
DEC_BATCH = 128
DEC_SEQ = 1
PAST_LEN = 16384
PAGE_SIZE = 128

H_R = 4
DK_R = 256
DV_R = 512
QR = H_R * DK_R
VR = H_R * DV_R
D_A = 2048
G_A = 8
CHUNK = 128
D_FF = ((8 * D_MODEL // 3 + 255) // 256) * 256
D_PLE = 256
ROPE_BASE = 10000.0
EPS = 1e-6
D_IN = 2 * QR + 2 * VR + 2 * D_A + 2 * D_MODEL
SPLITS = [QR, 2 * QR, 2 * QR + VR, 2 * QR + 2 * VR, 2 * QR + 2 * VR + D_A, 2 * QR + 2 * VR + 2 * D_A]

kernel_name = "hybrid_retention_chunkgmlp_decoder_step"


def rms_norm(x, g):
    x32 = x.astype(jnp.float32)
    y = x32 * lax.rsqrt(jnp.mean(x32 * x32, axis=-1, keepdims=True) + EPS) * g.astype(jnp.float32)
    return y.astype(x.dtype)


def layer_norm(x, g, b):
    x32 = x.astype(jnp.float32)
    mu = jnp.mean(x32, axis=-1, keepdims=True)
    var = jnp.mean(jnp.square(x32 - mu), axis=-1, keepdims=True)
    return (x32 - mu) * lax.rsqrt(var + EPS) * g.astype(jnp.float32) + b.astype(jnp.float32)


def rope(x, pos):
    half = x.shape[-1] // 2
    freqs = ROPE_BASE ** (-jnp.arange(half, dtype=jnp.float32) / half)
    ang = pos[:, None] * freqs[None, :]
    cos = jnp.cos(ang)[None, :, None, :]
    sin = jnp.sin(ang)[None, :, None, :]
    x1, x2 = x[..., :half], x[..., half:]
    return jnp.concatenate([x1 * cos - x2 * sin, x1 * sin + x2 * cos], axis=-1)


def retention(q, k, v, s0):
    B, T = q.shape[0], q.shape[1]
    L = min(T, CHUNK)
    n = T // L
    lg = jnp.log1p(-jnp.exp2(-5.0 - jnp.arange(H_R, dtype=jnp.float32)))
    idx = jnp.arange(L, dtype=jnp.float32)
    diff = idx[:, None] - idx[None, :]
    decay = jnp.where(diff >= 0, jnp.exp(lg[:, None, None] * jnp.maximum(diff, 0.0)), 0.0)
    q_dec = jnp.exp(lg[None, :] * (idx[:, None] + 1.0))
    k_dec = jnp.exp(lg[None, :] * (L - 1.0 - idx[:, None]))
    s_dec = jnp.exp(lg * L)

    def to_chunks(a):
        return a.reshape(B, n, L, *a.shape[2:]).swapaxes(0, 1)

    def step(s, qkv):
        qc, kc, vc = qkv
        scores = jnp.einsum('bihd,bjhd->bhij', qc, kc) * decay[None]
        o = (jnp.einsum('bhij,bjhv->bihv', scores, vc)
             + jnp.einsum('bihd,bhdv->bihv', qc, s) * q_dec[None, :, :, None])
        s = s * s_dec[None, :, None, None] + jnp.einsum('bjhd,bjhv->bhdv', kc * k_dec[None, :, :, None], vc)
        return s, o

    s_fin, o = lax.scan(step, s0.astype(jnp.float32), (to_chunks(q), to_chunks(k), to_chunks(v)))
    return o.swapaxes(0, 1).reshape(B, T, H_R, DV_R), s_fin


def chunk_spatial_mix(va, w_s, b_s):
    B, T = va.shape[0], va.shape[1]
    L = min(T, CHUNK)
    n = T // L
    v5 = va.reshape(B, n, L, G_A, D_A // G_A)
    w = w_s[:, :L, :L] * jnp.tril(jnp.ones((L, L), w_s.dtype))[None]
    s = jnp.einsum('gij,bnjgc->bnigc', w, v5) + b_s[:, :L].T[None, None, :, :, None]
    return s.reshape(B, T, D_A)


def layer(x, p, pos, s0, g_mix, w_in, w_ret_out, ln_a_g, ln_a_b, w_s, b_s, w_a_out, b_gate, w_o,
          g_ffn, w_ff_gate, w_ff_up, w_ff_down, g_ple, w_ple, w_ple_gate):
    B, T = x.shape[0], x.shape[1]
    xn = rms_norm(x, g_mix)
    z = xn @ w_in
    q_r, k_r, v_r, g_r, u_a, v_a, gates = jnp.split(z, SPLITS, axis=-1)
    q = rope(q_r.reshape(B, T, H_R, DK_R).astype(jnp.float32), pos)
    k = rope(k_r.reshape(B, T, H_R, DK_R).astype(jnp.float32), pos) * (DK_R ** -0.5)
    v = v_r.reshape(B, T, H_R, DV_R).astype(jnp.float32)
    o, s_new = retention(q, k, v, s0)
    mu = jnp.mean(o, axis=-1, keepdims=True)
    var = jnp.mean(jnp.square(o - mu), axis=-1, keepdims=True)
    o = ((o - mu) * lax.rsqrt(var + EPS)).reshape(B, T, VR)
    out_b = (jax.nn.silu(g_r.astype(jnp.float32)) * o).astype(x.dtype) @ w_ret_out
    u = jax.nn.gelu(u_a)
    va = layer_norm(jax.nn.gelu(v_a), ln_a_g, ln_a_b).astype(x.dtype)
    s_sp = chunk_spatial_mix(va, w_s, b_s)
    out_a = (u * s_sp) @ w_a_out
    gt = jax.nn.sigmoid(gates + b_gate)
    g_a, g_b = gt[..., :D_MODEL], gt[..., D_MODEL:]
    h = x + (g_a * out_a + g_b * out_b) @ w_o
    hn = rms_norm(h, g_ffn)
    h = h + (jax.nn.silu(hn @ w_ff_gate) * (hn @ w_ff_up)) @ w_ff_down
    h = h + jax.nn.sigmoid(rms_norm(h, g_ple) @ w_ple_gate) * (p @ w_ple)
    return h, s_new, va


def setup_inputs(seed: int = 0) -> dict:
    key = jax.random.key(seed)
    ks = jax.random.split(key, 32)
    f32 = jnp.float32

    def nrm(k, shape, scale):
        return jax.random.normal(k, shape, f32) * scale

    def gain(k, shape):
        return 1.0 + 0.01 * jax.random.normal(k, shape, f32)

    return {
        "x_prompt": nrm(ks[0], (BATCH, SEQ, D_MODEL), 1.0),
        "x_sample": nrm(ks[1], (DEC_BATCH, DEC_SEQ, D_MODEL), 1.0),
        "state_ret": nrm(ks[2], (DEPTH, DEC_BATCH, H_R, DK_R, DV_R), 0.1),
        "p_prompt": nrm(ks[3], (DEPTH, BATCH, SEQ, D_PLE), 1.0),
        "p_sample": nrm(ks[4], (DEPTH, DEC_BATCH, DEC_SEQ, D_PLE), 1.0),
        "g_mix": gain(ks[5], (DEPTH, D_MODEL)),
        "w_in": nrm(ks[6], (DEPTH, D_MODEL, D_IN), D_MODEL ** -0.5),
        "w_ret_out": nrm(ks[7], (DEPTH, VR, D_MODEL), VR ** -0.5),
        "ln_a_g": gain(ks[8], (DEPTH, D_A)),
        "ln_a_b": nrm(ks[9], (DEPTH, D_A), 0.01),
        "w_s": nrm(ks[10], (DEPTH, G_A, CHUNK, CHUNK), 0.5 * CHUNK ** -0.5),
        "b_s": gain(ks[11], (DEPTH, G_A, CHUNK)),
        "w_a_out": nrm(ks[12], (DEPTH, D_A, D_MODEL), D_A ** -0.5),
        "b_gate": nrm(ks[13], (DEPTH, 2 * D_MODEL), 0.01),
        "w_o": nrm(ks[14], (DEPTH, D_MODEL, D_MODEL), D_MODEL ** -0.5),
        "g_ffn": gain(ks[15], (DEPTH, D_MODEL)),
        "w_ff_gate": nrm(ks[16], (DEPTH, D_MODEL, D_FF), D_MODEL ** -0.5),
        "w_ff_up": nrm(ks[17], (DEPTH, D_MODEL, D_FF), D_MODEL ** -0.5),
        "w_ff_down": nrm(ks[18], (DEPTH, D_FF, D_MODEL), D_FF ** -0.5),
        "g_ple": gain(ks[19], (DEPTH, D_MODEL)),
        "w_ple": nrm(ks[20], (DEPTH, D_PLE, D_MODEL), D_PLE ** -0.5),
        "w_ple_gate": nrm(ks[21], (DEPTH, D_MODEL, D_MODEL), D_MODEL ** -0.5),
        "g_final": gain(ks[22], (D_MODEL,)),
    }


def reference(x_prompt, x_sample, state_ret, p_prompt, p_sample, g_mix, w_in, w_ret_out, ln_a_g, ln_a_b,
              w_s, b_s, w_a_out, b_gate, w_o, g_ffn, w_ff_gate, w_ff_up, w_ff_down, g_ple, w_ple,
              w_ple_gate, g_final):
    pos_p = jnp.arange(SEQ, dtype=jnp.float32)
    pos_s = PAST_LEN + jnp.arange(DEC_SEQ, dtype=jnp.float32)
    s0_prompt = jnp.zeros((BATCH, H_R, DK_R, DV_R), jnp.float32)
    h_p, h_s = x_prompt, x_sample
    st_p, st_s, vrows_s = [], [], []
    for i in range(DEPTH):
        lw = (g_mix[i], w_in[i], w_ret_out[i], ln_a_g[i], ln_a_b[i], w_s[i], b_s[i], w_a_out[i], b_gate[i],
              w_o[i], g_ffn[i], w_ff_gate[i], w_ff_up[i], w_ff_down[i], g_ple[i], w_ple[i], w_ple_gate[i])
        h_p, sp, _ = layer(h_p, p_prompt[i], pos_p, s0_prompt, *lw)
        h_s, ss, vs = layer(h_s, p_sample[i], pos_s, state_ret[i], *lw)
        st_p.append(sp)
        st_s.append(ss)
        vrows_s.append(vs)
    y_prompt = rms_norm(h_p, g_final)
    y_sample = rms_norm(h_s, g_final)
    state_ret_prompt = jnp.stack(st_p, axis=0)
    state_ret_sample = jnp.stack(st_s, axis=0)
    chunk_v_sample = jnp.stack(vrows_s, axis=0)
    return (y_prompt, y_sample, state_ret_prompt, state_ret_sample, chunk_v_sample)
```

```python
import functools

import jax
import jax.numpy as jnp
from jax import lax
from jax.experimental import pallas as pl
from jax.experimental.pallas import tpu as pltpu

D_MODEL = 1024
H_R = 4
DK_R = 256
DV_R = 512
QR = H_R * DK_R
VR = H_R * DV_R
D_A = 2048
G_A = 8
GC_A = D_A // G_A
CHUNK = 128
D_PLE = 256
PAST_LEN = 16384
ROPE_BASE = 10000.0
EPS = 1e-6
HALF = DK_R // 2

VMEM_LIMIT_BYTES = 56 * 1024 * 1024
TOKEN_TILE = 512
FF_BLOCK = 512
STATE_ROWS = 2

F32 = jnp.float32
BF16 = jnp.bfloat16


def _mm(a, b):
    return jnp.dot(a.astype(BF16), b.astype(BF16), preferred_element_type=F32)


def _mm_nt(a, b):
    return lax.dot_general(a.astype(BF16), b.astype(BF16), (((1,), (1,)), ((), ())),
                           preferred_element_type=F32)


def _rms(x, g):
    return x * lax.rsqrt(jnp.mean(x * x, axis=-1, keepdims=True) + EPS) * g


def _gelu(x):
    c = 0.7978845608028654
    return x * (0.5 * (1.0 + jnp.tanh(c * (x + 0.044715 * (x * x * x)))))


def _sigmoid(x):
    return 1.0 / (1.0 + jnp.exp(-x))


def _silu(x):
    return x * _sigmoid(x)


def _rope(x, cos, sin):
    x1, x2 = x[:, :HALF], x[:, HALF:]
    return jnp.concatenate([x1 * cos - x2 * sin, x1 * sin + x2 * cos], axis=-1)


def _group_norm(o):
    mu = jnp.mean(o, axis=-1, keepdims=True)
    d = o - mu
    var = jnp.mean(d * d, axis=-1, keepdims=True)
    return d * lax.rsqrt(var + EPS)


def _layer_norm(x, g, b):
    mu = jnp.mean(x, axis=-1, keepdims=True)
    d = x - mu
    var = jnp.mean(d * d, axis=-1, keepdims=True)
    return d * lax.rsqrt(var + EPS) * g + b


def _full(shape):
    n = len(shape)
    return pl.BlockSpec(shape, lambda *_: (0,) * n)


def _smem():
    return pl.BlockSpec(memory_space=pltpu.SMEM)


def _ret_prompt_kernel(sdec_ref, x_ref, gmix_ref, wq_ref, wk_ref, wv_ref, wg_ref, cos_ref, sin_ref,
                       decay_ref, qdec_ref, kdec_ref, wro_ref, outb_ref, s_ref, og_scr):
    t = pl.program_id(1)

    @pl.when(t == 0)
    def _():
        s_ref[...] = jnp.zeros_like(s_ref)

    tm = x_ref.shape[1]
    xn = _rms(x_ref[0], gmix_ref[...]).astype(BF16)
    cos = cos_ref[...]
    sin = sin_ref[...]
    for h in range(H_R):
        q = _rope(_mm(xn, wq_ref[:, h * DK_R:(h + 1) * DK_R]), cos, sin)
        k = _rope(_mm(xn, wk_ref[:, h * DK_R:(h + 1) * DK_R]), cos, sin) * (DK_R ** -0.5)
        v = _mm(xn, wv_ref[:, h * DV_R:(h + 1) * DV_R])
        g = _mm(xn, wg_ref[:, h * DV_R:(h + 1) * DV_R])
        for c in range(tm // CHUNK):
            r = slice(c * CHUNK, (c + 1) * CHUNK)
            qc, kc, vc = q[r], k[r], v[r]
            s_old = s_ref[0, 0, h]
            scores = _mm_nt(qc, kc) * decay_ref[h]
            o = _mm(scores, vc) + _mm(qc, s_old) * qdec_ref[h]
            kd_t = (kc * kdec_ref[h]).T
            s_ref[0, 0, h] = s_old * sdec_ref[h] + _mm(kd_t, vc)
            og_scr[r, h * DV_R:(h + 1) * DV_R] = (_silu(g[r]) * _group_norm(o)).astype(BF16)
    outb_ref[...] = _mm(og_scr[...], wro_ref[...]).astype(outb_ref.dtype)


def _ret_prompt(x, g_mix, wq, wk, wv, wg, cos, sin, decay, qdec, kdec, sdec, w_ret_out):
    b, seq, _ = x.shape
    tm = TOKEN_TILE
    nt = seq // tm
    return pl.pallas_call(
        _ret_prompt_kernel,
        grid=(b, nt),
        in_specs=[
            _smem(),
            pl.BlockSpec((1, tm, D_MODEL), lambda i, t: (i, t, 0)),
            _full((1, D_MODEL)),
            _full(wq.shape), _full(wk.shape), _full(wv.shape), _full(wg.shape),
            pl.BlockSpec((tm, HALF), lambda i, t: (t, 0)),
            pl.BlockSpec((tm, HALF), lambda i, t: (t, 0)),
            _full(decay.shape), _full(qdec.shape), _full(kdec.shape),
            _full(w_ret_out.shape),
        ],
        out_specs=[
            pl.BlockSpec((tm, D_MODEL), lambda i, t: (i * nt + t, 0)),
            pl.BlockSpec((1, 1, H_R, DK_R, DV_R), lambda i, t: (0, i, 0, 0, 0)),
        ],
        out_shape=[
            jax.ShapeDtypeStruct((b * seq, D_MODEL), BF16),
            jax.ShapeDtypeStruct((1, b, H_R, DK_R, DV_R), F32),
        ],
        scratch_shapes=[pltpu.VMEM((tm, VR), BF16)],
        compiler_params=pltpu.CompilerParams(
            dimension_semantics=("arbitrary", "arbitrary"), vmem_limit_bytes=VMEM_LIMIT_BYTES),
        name="ret_prompt",
    )(sdec, x, g_mix, wq, wk, wv, wg, cos, sin, decay, qdec, kdec, w_ret_out)


def _sgu_prompt_kernel(x_ref, gmix_ref, wu_ref, wva_ref, lng_ref, lnb_ref, ws_ref, bs_ref, wao_ref,
                       outa_ref, us_scr):
    tm = x_ref.shape[0]
    xn = _rms(x_ref[...], gmix_ref[...]).astype(BF16)
    va = _layer_norm(_gelu(_mm(xn, wva_ref[...])), lng_ref[...], lnb_ref[...]).astype(BF16)
    row = lax.broadcasted_iota(jnp.int32, (CHUNK, CHUNK), 0)
    col = lax.broadcasted_iota(jnp.int32, (CHUNK, CHUNK), 1)
    causal = row >= col
    for gi in range(G_A):
        cs = slice(gi * GC_A, (gi + 1) * GC_A)
        w = jnp.where(causal, ws_ref[gi], 0.0).astype(BF16)
        u = _gelu(_mm(xn, wu_ref[:, cs]))
        for c in range(tm // CHUNK):
            r = slice(c * CHUNK, (c + 1) * CHUNK)
            s = jnp.dot(w, va[r, cs], preferred_element_type=F32) + bs_ref[gi]
            us_scr[r, cs] = (u[r] * s).astype(BF16)
    outa_ref[...] = _mm(us_scr[...], wao_ref[...]).astype(outa_ref.dtype)


def _sgu_prompt(x2, g_mix, wu, wva, ln_g, ln_b, w_s, b_s3, w_a_out):
    n = x2.shape[0]
    tm = TOKEN_TILE
    return pl.pallas_call(
        _sgu_prompt_kernel,
        grid=(n // tm,),
        in_specs=[
            pl.BlockSpec((tm, D_MODEL), lambda t: (t, 0)),
            _full((1, D_MODEL)),
            _full(wu.shape), _full(wva.shape), _full((1, D_A)), _full((1, D_A)),
            _full(w_s.shape), _full(b_s3.shape), _full(w_a_out.shape),
        ],
        out_specs=pl.BlockSpec((tm, D_MODEL), lambda t: (t, 0)),
        out_shape=jax.ShapeDtypeStruct((n, D_MODEL), BF16),
        scratch_shapes=[pltpu.VMEM((tm, D_A), BF16)],
        compiler_params=pltpu.CompilerParams(
            dimension_semantics=("arbitrary",), vmem_limit_bytes=VMEM_LIMIT_BYTES),
        name="sgu_prompt",
    )(x2, g_mix, wu, wva, ln_g, ln_b, w_s, b_s3, w_a_out)


def _mix_kernel(x_ref, oa_ref, ob_ref, p_ref, gmix_ref, wgt_ref, bgt_ref, wo_ref, gffn_ref, wfg_ref,
                wfu_ref, wfd_ref, gple_ref, wple_ref, wpg_ref, gfin_ref, y_ref):
    x = x_ref[...]
    xn = _rms(x, gmix_ref[...]).astype(BF16)
    gt = _sigmoid(_mm(xn, wgt_ref[...]) + bgt_ref[...])
    m = gt[:, :D_MODEL] * oa_ref[...].astype(F32) + gt[:, D_MODEL:] * ob_ref[...].astype(F32)
    h = x + _mm(m, wo_ref[...])
    hn = _rms(h, gffn_ref[...]).astype(BF16)
    d_ff = wfg_ref.shape[1]
    ff = None
    for lo in range(0, d_ff, FF_BLOCK):
        hi = min(lo + FF_BLOCK, d_ff)
        act = _silu(_mm(hn, wfg_ref[:, lo:hi])) * _mm(hn, wfu_ref[:, lo:hi])
        part = _mm(act, wfd_ref[lo:hi, :])
        ff = part if ff is None else ff + part
    h = h + ff
    hp = _rms(h, gple_ref[...]).astype(BF16)
    h = h + _sigmoid(_mm(hp, wpg_ref[...])) * _mm(p_ref[...], wple_ref[...])
    y_ref[...] = _rms(h, gfin_ref[...])


def _mix(x2, oa, ob, p2, g_mix, wgt, b_gate, w_o, g_ffn, wfg, wfu, wfd, g_ple, w_ple, w_ple_gate,
         g_final, tm):
    n = x2.shape[0]
    return pl.pallas_call(
        _mix_kernel,
        grid=(n // tm,),
        in_specs=[
            pl.BlockSpec((tm, D_MODEL), lambda t: (t, 0)),
            pl.BlockSpec((tm, D_MODEL), lambda t: (t, 0)),
            pl.BlockSpec((tm, D_MODEL), lambda t: (t, 0)),
            pl.BlockSpec((tm, D_PLE), lambda t: (t, 0)),
            _full((1, D_MODEL)),
            _full(wgt.shape), _full((1, 2 * D_MODEL)), _full(w_o.shape), _full((1, D_MODEL)),
            _full(wfg.shape), _full(wfu.shape), _full(wfd.shape), _full((1, D_MODEL)),
            _full(w_ple.shape), _full(w_ple_gate.shape), _full((1, D_MODEL)),
        ],
        out_specs=pl.BlockSpec((tm, D_MODEL), lambda t: (t, 0)),
        out_shape=jax.ShapeDtypeStruct((n, D_MODEL), F32),
        compiler_params=pltpu.CompilerParams(
            dimension_semantics=("arbitrary",), vmem_limit_bytes=VMEM_LIMIT_BYTES),
        name="mix_ffn",
    )(x2, oa, ob, p2, g_mix, wgt, b_gate, w_o, g_ffn, wfg, wfu, wfd, g_ple, w_ple, w_ple_gate, g_final)


def _qkv_sample_kernel(x_ref, gmix_ref, wq_ref, wk_ref, wv_ref, cos_ref, sin_ref, q_ref, k_ref, v_ref):
    xn = _rms(x_ref[...], gmix_ref[...]).astype(BF16)
    cos = cos_ref[...]
    sin = sin_ref[...]
    for h in range(H_R):
        hs = slice(h * DK_R, (h + 1) * DK_R)
        q_ref[:, hs] = _rope(_mm(xn, wq_ref[:, hs]), cos, sin)
        k_ref[:, hs] = _rope(_mm(xn, wk_ref[:, hs]), cos, sin) * (DK_R ** -0.5)
    v_ref[...] = _mm(xn, wv_ref[...])


def _qkv_sample(x2, g_mix, wq, wk, wv, cos, sin):
    n = x2.shape[0]
    return pl.pallas_call(
        _qkv_sample_kernel,
        grid=(1,),
        in_specs=[_full(x2.shape), _full((1, D_MODEL)), _full(wq.shape), _full(wk.shape),
                  _full(wv.shape), _full(cos.shape), _full(sin.shape)],
        out_specs=[_full((n, QR)), _full((n, QR)), _full((n, VR))],
        out_shape=[jax.ShapeDtypeStruct((n, QR), F32), jax.ShapeDtypeStruct((n, QR), F32),
                   jax.ShapeDtypeStruct((n, VR), F32)],
        compiler_params=pltpu.CompilerParams(
            dimension_semantics=("arbitrary",), vmem_limit_bytes=VMEM_LIMIT_BYTES),
        name="qkv_sample",
    )(x2, g_mix, wq, wk, wv, cos, sin)


def _state_sample_kernel(dec_ref, q_ref, k_ref, v_ref, s_ref, o_ref, snew_ref):
    for j in range(STATE_ROWS):
        qrow = q_ref[0, j:j + 1, :]
        krow = k_ref[0, j:j + 1, :]
        rows = [qrow[:, h * DK_R:(h + 1) * DK_R] for h in range(H_R)]
        rows += [krow[:, h * DK_R:(h + 1) * DK_R] for h in range(H_R)]
        cols = jnp.concatenate(rows, axis=0).T
        for h in range(H_R):
            qcol = cols[:, h:h + 1]
            kcol = cols[:, H_R + h:H_R + h + 1] * dec_ref[2, h]
            vrow = v_ref[0, j:j + 1, h * DV_R:(h + 1) * DV_R]
            s_old = s_ref[0, j, h]
            score = jnp.sum(rows[h] * rows[H_R + h], axis=-1, keepdims=True) * dec_ref[0, h]
            inter = jnp.sum(qcol * s_old, axis=0, keepdims=True) * dec_ref[1, h]
            o_ref[0, j:j + 1, h * DV_R:(h + 1) * DV_R] = score * vrow + inter
            snew_ref[0, j, h] = s_old * dec_ref[3, h] + kcol * vrow


def _state_sample(dec, q3, k3, v3, state):
    nb = q3.shape[0]
    return pl.pallas_call(
        _state_sample_kernel,
        grid=(nb,),
        in_specs=[
            _smem(),
            pl.BlockSpec((1, STATE_ROWS, QR), lambda i: (i, 0, 0)),
            pl.BlockSpec((1, STATE_ROWS, QR), lambda i: (i, 0, 0)),
            pl.BlockSpec((1, STATE_ROWS, VR), lambda i: (i, 0, 0)),
            pl.BlockSpec((1, STATE_ROWS, H_R, DK_R, DV_R), lambda i: (0, i, 0, 0, 0)),
        ],
        out_specs=[
            pl.BlockSpec((1, STATE_ROWS, VR), lambda i: (i, 0, 0)),
            pl.BlockSpec((1, STATE_ROWS, H_R, DK_R, DV_R), lambda i: (0, i, 0, 0, 0)),
        ],
        out_shape=[
            jax.ShapeDtypeStruct((nb, STATE_ROWS, VR), F32),
            jax.ShapeDtypeStruct(state.shape, F32),
        ],
        compiler_params=pltpu.CompilerParams(
            dimension_semantics=("arbitrary",), vmem_limit_bytes=VMEM_LIMIT_BYTES),
        name="state_sample",
    )(dec, q3, k3, v3, state)


def _branches_sample_kernel(x_ref, o_ref, gmix_ref, wg_ref, wro_ref, wu_ref, wva_ref, lng_ref, lnb_ref,
                            w00_ref, b0_ref, wao_ref, outa_ref, outb_ref, va_ref):
    xn = _rms(x_ref[...], gmix_ref[...]).astype(BF16)
    g = _mm(xn, wg_ref[...])
    o = o_ref[...]
    gated = [_silu(g[:, h * DV_R:(h + 1) * DV_R]) * _group_norm(o[:, h * DV_R:(h + 1) * DV_R])
             for h in range(H_R)]
    outb_ref[...] = _mm(jnp.concatenate(gated, axis=-1), wro_ref[...]).astype(outb_ref.dtype)
    u = _gelu(_mm(xn, wu_ref[...]))
    va = _layer_norm(_gelu(_mm(xn, wva_ref[...])), lng_ref[...], lnb_ref[...])
    va_ref[...] = va
    s = w00_ref[...] * va + b0_ref[...]
    outa_ref[...] = _mm(u * s, wao_ref[...]).astype(outa_ref.dtype)


def _branches_sample(x2, o_raw, g_mix, wg, w_ret_out, wu, wva, ln_g, ln_b, w00, b0, w_a_out):
    n = x2.shape[0]
    return pl.pallas_call(
        _branches_sample_kernel,
        grid=(1,),
        in_specs=[_full(x2.shape), _full(o_raw.shape), _full((1, D_MODEL)), _full(wg.shape),
                  _full(w_ret_out.shape), _full(wu.shape), _full(wva.shape), _full((1, D_A)),
                  _full((1, D_A)), _full((1, D_A)), _full((1, D_A)), _full(w_a_out.shape)],
        out_specs=[_full((n, D_MODEL)), _full((n, D_MODEL)), _full((n, D_A))],
        out_shape=[jax.ShapeDtypeStruct((n, D_MODEL), BF16), jax.ShapeDtypeStruct((n, D_MODEL), BF16),
                   jax.ShapeDtypeStruct((n, D_A), F32)],
        compiler_params=pltpu.CompilerParams(
            dimension_semantics=("arbitrary",), vmem_limit_bytes=VMEM_LIMIT_BYTES),
        name="branches_sample",
    )(x2, o_raw, g_mix, wg, w_ret_out, wu, wva, ln_g, ln_b, w00, b0, w_a_out)


def _rope_tables(pos):
    freqs = ROPE_BASE ** (-jnp.arange(HALF, dtype=F32) / HALF)
    ang = pos[:, None] * freqs[None, :]
    return jnp.cos(ang), jnp.sin(ang)


def _decay_tables(length):
    lg = jnp.log1p(-jnp.exp2(-5.0 - jnp.arange(H_R, dtype=F32)))
    idx = jnp.arange(length, dtype=F32)
    diff = idx[:, None] - idx[None, :]
    decay = jnp.where(diff >= 0, jnp.exp(lg[:, None, None] * jnp.maximum(diff, 0.0)), 0.0)
    q_dec = jnp.exp(lg[None, :] * (idx[:, None] + 1.0))
    k_dec = jnp.exp(lg[None, :] * (length - 1.0 - idx[:, None]))
    s_dec = jnp.exp(lg * length)
    return decay, q_dec, k_dec, s_dec


def kernel(x_prompt, x_sample, state_ret, p_prompt, p_sample, g_mix, w_in, w_ret_out, ln_a_g, ln_a_b,
           w_s, b_s, w_a_out, b_gate, w_o, g_ffn, w_ff_gate, w_ff_up, w_ff_down, g_ple, w_ple,
           w_ple_gate, g_final):
    depth = w_in.shape[0]
    assert depth == 1, "single-layer step"
    b, seq, _ = x_prompt.shape
    nb, dec_seq, _ = x_sample.shape
    assert dec_seq == 1 and seq % TOKEN_TILE == 0 and TOKEN_TILE % CHUNK == 0

    i = 0
    w = w_in[i].astype(BF16)
    wq, wk = w[:, :QR], w[:, QR:2 * QR]
    wv, wg = w[:, 2 * QR:2 * QR + VR], w[:, 2 * QR + VR:2 * QR + 2 * VR]
    o0 = 2 * QR + 2 * VR
    wu, wva, wgt = w[:, o0:o0 + D_A], w[:, o0 + D_A:o0 + 2 * D_A], w[:, o0 + 2 * D_A:]
    wro = w_ret_out[i].astype(BF16)
    wao = w_a_out[i].astype(BF16)
    wo = w_o[i].astype(BF16)
    wfg, wfu, wfd = w_ff_gate[i].astype(BF16), w_ff_up[i].astype(BF16), w_ff_down[i].astype(BF16)
    wple, wpg = w_ple[i].astype(BF16), w_ple_gate[i].astype(BF16)
    gmix = g_mix[i][None, :]
    lng, lnb = ln_a_g[i][None, :], ln_a_b[i][None, :]
    bgt = b_gate[i][None, :]
    gffn, gple, gfin = g_ffn[i][None, :], g_ple[i][None, :], g_final[None, :]

    cos_p, sin_p = _rope_tables(jnp.arange(seq, dtype=F32))
    decay, q_dec, k_dec, s_dec = _decay_tables(CHUNK)
    qdec_full = jnp.broadcast_to(q_dec.T[:, :, None], (H_R, CHUNK, DV_R))
    kdec_full = jnp.broadcast_to(k_dec.T[:, :, None], (H_R, CHUNK, DK_R))
    x2 = x_prompt.reshape(b * seq, D_MODEL)
    out_b, st_p = _ret_prompt(x_prompt, gmix, wq, wk, wv, wg, cos_p, sin_p, decay, qdec_full, kdec_full,
                              s_dec, wro)
    out_a = _sgu_prompt(x2, gmix, wu, wva, lng, lnb, w_s[i], b_s[i][:, :, None], wao)
    y_p = _mix(x2, out_a, out_b, p_prompt[i].reshape(b * seq, D_PLE), gmix, wgt, bgt, wo, gffn, wfg, wfu,
               wfd, gple, wple, wpg, gfin, TOKEN_TILE)

    cos_s, sin_s = _rope_tables(PAST_LEN + jnp.arange(dec_seq, dtype=F32))
    decay1, q_dec1, k_dec1, s_dec1 = _decay_tables(1)
    dec1 = jnp.stack([decay1[:, 0, 0], q_dec1[0], k_dec1[0], s_dec1], axis=0)
    xs2 = x_sample.reshape(nb, D_MODEL)
    q_s, k_s, v_s = _qkv_sample(xs2, gmix, wq, wk, wv, cos_s, sin_s)
    o_raw, st_s = _state_sample(dec1, q_s.reshape(nb // STATE_ROWS, STATE_ROWS, QR),
                                k_s.reshape(nb // STATE_ROWS, STATE_ROWS, QR),
                                v_s.reshape(nb // STATE_ROWS, STATE_ROWS, VR), state_ret)
    w00 = jnp.repeat(w_s[i][:, 0, 0], GC_A)[None, :]
    b0 = jnp.repeat(b_s[i][:, 0], GC_A)[None, :]
    oa_s, ob_s, va_s = _branches_sample(xs2, o_raw.reshape(nb, VR), gmix, wg, wro, wu, wva, lng, lnb,
                                        w00, b0, wao)
    y_s = _mix(xs2, oa_s, ob_s, p_sample[i].reshape(nb, D_PLE), gmix, wgt, bgt, wo, gffn, wfg, wfu, wfd,
               gple, wple, wpg, gfin, nb)

    return (y_p.reshape(b, seq, D_MODEL), y_s.reshape(nb, dec_seq, D_MODEL), st_p, st_s,
            va_s.reshape(1, nb, dec_seq, D_A))
```

```python
import functools

import jax
import jax.numpy as jnp
from jax import lax
from jax.experimental import pallas as pl
from jax.experimental.pallas import tpu as pltpu

D_MODEL = 1024
H_R = 4
DK_R = 256
DV_R = 512
QR = H_R * DK_R
VR = H_R * DV_R
D_A = 2048
G_A = 8
GC_A = D_A // G_A
CHUNK = 128
RET_CHUNK = 256
D_PLE = 256
PAST_LEN = 16384
ROPE_BASE = 10000.0
EPS = 1e-6
HALF = DK_R // 2
COL_Q, COL_K, COL_V, COL_G = 0, QR, 2 * QR, 2 * QR + VR
COL_U, COL_VA, COL_GATES = 2 * QR + 2 * VR, 2 * QR + 2 * VR + D_A, 2 * QR + 2 * VR + 2 * D_A

VMEM_LIMIT_BYTES = 56 * 1024 * 1024
TOKEN_TILE = 512
FF_BLOCK = 512
STATE_ROWS = 2
MIX_TILE = 256

F32 = jnp.float32
BF16 = jnp.bfloat16


def _mm(a, b):
    return jnp.dot(a.astype(BF16), b.astype(BF16), preferred_element_type=F32)


def _mm_nt(a, b):
    return lax.dot_general(a.astype(BF16), b.astype(BF16), (((1,), (1,)), ((), ())),
                           preferred_element_type=F32)


def _rms(x, g):
    return x * lax.rsqrt(jnp.mean(x * x, axis=-1, keepdims=True) + EPS) * g


def _gelu(x):
    c = 0.7978845608028654
    return x * (0.5 * (1.0 + jnp.tanh(c * (x + 0.044715 * (x * x * x)))))


def _sigmoid(x):
    return 1.0 / (1.0 + jnp.exp(-x))


def _silu(x):
    return x * _sigmoid(x)


def _rope(x, cos, sin):
    x1, x2 = x[:, :HALF], x[:, HALF:]
    return jnp.concatenate([x1 * cos - x2 * sin, x1 * sin + x2 * cos], axis=-1)


def _group_norm(o):
    mu = jnp.mean(o, axis=-1, keepdims=True)
    d = o - mu
    var = jnp.mean(d * d, axis=-1, keepdims=True)
    return d * lax.rsqrt(var + EPS)


def _layer_norm(x, g, b):
    mu = jnp.mean(x, axis=-1, keepdims=True)
    d = x - mu
    var = jnp.mean(d * d, axis=-1, keepdims=True)
    return d * lax.rsqrt(var + EPS) * g + b


def _full(shape):
    n = len(shape)
    return pl.BlockSpec(shape, lambda *_: (0,) * n)


def _w_cols(start, width):
    assert start % width == 0
    return pl.BlockSpec((D_MODEL, width), lambda *_: (0, start // width), pipeline_mode=pl.Buffered(1))


def _smem():
    return pl.BlockSpec(memory_space=pltpu.SMEM)


def _ret_prompt_kernel(sdec_ref, x_ref, gmix_ref, wq_ref, wk_ref, wv_ref, wg_ref, cos_ref, sin_ref,
                       decay_ref, qdec_ref, kdec_ref, wro_ref, outb_ref, s_ref, og_scr):
    t = pl.program_id(1)

    @pl.when(t == 0)
    def _():
        s_ref[...] = jnp.zeros_like(s_ref)

    tm = x_ref.shape[1]
    xn = _rms(x_ref[0], gmix_ref[...]).astype(BF16)
    cos = cos_ref[...]
    sin = sin_ref[...]
    for h in range(H_R):
        q = _rope(_mm(xn, wq_ref[:, h * DK_R:(h + 1) * DK_R]), cos, sin)
        k = _rope(_mm(xn, wk_ref[:, h * DK_R:(h + 1) * DK_R]), cos, sin) * (DK_R ** -0.5)
        v = _mm(xn, wv_ref[:, h * DV_R:(h + 1) * DV_R])
        g = _mm(xn, wg_ref[:, h * DV_R:(h + 1) * DV_R])
        for c in range(tm // RET_CHUNK):
            r = slice(c * RET_CHUNK, (c + 1) * RET_CHUNK)
            qc, kc, vc = q[r], k[r], v[r]
            s_old = s_ref[0, 0, h]
            scores = _mm_nt(qc, kc) * decay_ref[h]
            o = _mm(scores, vc) + _mm(qc, s_old) * qdec_ref[h]
            kd_t = (kc * kdec_ref[h]).T
            s_ref[0, 0, h] = s_old * sdec_ref[h] + _mm(kd_t, vc)
            og_scr[r, h * DV_R:(h + 1) * DV_R] = (_silu(g[r]) * _group_norm(o)).astype(BF16)
    outb_ref[...] = _mm(og_scr[...], wro_ref[...]).astype(outb_ref.dtype)


def _ret_prompt(x, g_mix, w, cos, sin, decay, qdec, kdec, sdec, w_ret_out):
    b, seq, _ = x.shape
    tm = TOKEN_TILE
    nt = seq // tm
    return pl.pallas_call(
        _ret_prompt_kernel,
        grid=(b, nt),
        in_specs=[
            _smem(),
            pl.BlockSpec((1, tm, D_MODEL), lambda i, t: (i, t, 0)),
            _full((1, D_MODEL)),
            _w_cols(COL_Q, QR), _w_cols(COL_K, QR), _w_cols(COL_V, VR), _w_cols(COL_G, VR),
            pl.BlockSpec((tm, HALF), lambda i, t: (t, 0)),
            pl.BlockSpec((tm, HALF), lambda i, t: (t, 0)),
            _full(decay.shape), _full(qdec.shape), _full(kdec.shape),
            _full(w_ret_out.shape),
        ],
        out_specs=[
            pl.BlockSpec((tm, D_MODEL), lambda i, t: (i * nt + t, 0)),
            pl.BlockSpec((1, 1, H_R, DK_R, DV_R), lambda i, t: (0, i, 0, 0, 0)),
        ],
        out_shape=[
            jax.ShapeDtypeStruct((b * seq, D_MODEL), BF16),
            jax.ShapeDtypeStruct((1, b, H_R, DK_R, DV_R), F32),
        ],
        scratch_shapes=[pltpu.VMEM((tm, VR), BF16)],
        compiler_params=pltpu.CompilerParams(
            dimension_semantics=("arbitrary", "arbitrary"), vmem_limit_bytes=VMEM_LIMIT_BYTES),
        name="ret_prompt",
    )(sdec, x, g_mix, w, w, w, w, cos, sin, decay, qdec, kdec, w_ret_out)


def _sgu_prompt_kernel(x_ref, gmix_ref, wu_ref, wva_ref, lng_ref, lnb_ref, ws_ref, bs_ref, wao_ref,
                       outa_ref, us_scr):
    tm = x_ref.shape[0]
    xn = _rms(x_ref[...], gmix_ref[...]).astype(BF16)
    va = _layer_norm(_gelu(_mm(xn, wva_ref[...])), lng_ref[...], lnb_ref[...]).astype(BF16)
    row = lax.broadcasted_iota(jnp.int32, (CHUNK, CHUNK), 0)
    col = lax.broadcasted_iota(jnp.int32, (CHUNK, CHUNK), 1)
    causal = row >= col
    for gi in range(G_A):
        cs = slice(gi * GC_A, (gi + 1) * GC_A)
        w = jnp.where(causal, ws_ref[gi], 0.0).astype(BF16)
        u = _gelu(_mm(xn, wu_ref[:, cs]))
        for c in range(tm // CHUNK):
            r = slice(c * CHUNK, (c + 1) * CHUNK)
            s = jnp.dot(w, va[r, cs], preferred_element_type=F32) + bs_ref[gi]
            us_scr[r, cs] = (u[r] * s).astype(BF16)
    outa_ref[...] = _mm(us_scr[...], wao_ref[...]).astype(outa_ref.dtype)


def _sgu_prompt(x2, g_mix, w, ln_g, ln_b, w_s, b_s3, w_a_out):
    n = x2.shape[0]
    tm = TOKEN_TILE
    return pl.pallas_call(
        _sgu_prompt_kernel,
        grid=(n // tm,),
        in_specs=[
            pl.BlockSpec((tm, D_MODEL), lambda t: (t, 0)),
            _full((1, D_MODEL)),
            _w_cols(COL_U, D_A), _w_cols(COL_VA, D_A), _full((1, D_A)), _full((1, D_A)),
            _full(w_s.shape), _full(b_s3.shape), _full(w_a_out.shape),
        ],
        out_specs=pl.BlockSpec((tm, D_MODEL), lambda t: (t, 0)),
        out_shape=jax.ShapeDtypeStruct((n, D_MODEL), BF16),
        scratch_shapes=[pltpu.VMEM((tm, D_A), BF16)],
        compiler_params=pltpu.CompilerParams(
            dimension_semantics=("arbitrary",), vmem_limit_bytes=VMEM_LIMIT_BYTES),
        name="sgu_prompt",
    )(x2, g_mix, w, w, ln_g, ln_b, w_s, b_s3, w_a_out)


def _retention_token_step(dec_ref, q_ref, k_ref, v_ref, s_ref, o_ref, snew_ref):
    for j in range(STATE_ROWS):
        qrow = q_ref[0, j:j + 1, :]
        krow = k_ref[0, j:j + 1, :]
        rows = [qrow[:, h * DK_R:(h + 1) * DK_R] for h in range(H_R)]
        rows += [krow[:, h * DK_R:(h + 1) * DK_R] for h in range(H_R)]
        cols = jnp.concatenate(rows, axis=0).T
        for h in range(H_R):
            qcol = cols[:, h:h + 1]
            kcol = cols[:, H_R + h:H_R + h + 1] * dec_ref[2, h]
            vrow = v_ref[0, j:j + 1, h * DV_R:(h + 1) * DV_R]
            s_old = s_ref[0, j, h]
            score = jnp.sum(rows[h] * rows[H_R + h], axis=-1, keepdims=True) * dec_ref[0, h]
            inter = jnp.sum(qcol * s_old, axis=0, keepdims=True) * dec_ref[1, h]
            o_ref[0, j:j + 1, h * DV_R:(h + 1) * DV_R] = score * vrow + inter
            snew_ref[0, j, h] = s_old * dec_ref[3, h] + kcol * vrow


def _mix_kernel(*refs, with_state):
    if with_state:
        state_in, refs = refs[:5], refs[5:]
        refs, state_out = refs[:-2], refs[-2:]
        _retention_token_step(*state_in, *state_out)
    (x_ref, oa_ref, ob_ref, p_ref, gmix_ref, wgt_ref, bgt_ref, wo_ref, gffn_ref, wfg_ref, wfu_ref, wfd_ref,
     gple_ref, wple_ref, wpg_ref, gfin_ref, y_ref) = refs
    x = x_ref[...]
    xn = _rms(x, gmix_ref[...]).astype(BF16)
    gt = _sigmoid(_mm(xn, wgt_ref[...]) + bgt_ref[...])
    m = gt[:, :D_MODEL] * oa_ref[...].astype(F32) + gt[:, D_MODEL:] * ob_ref[...].astype(F32)
    h = x + _mm(m, wo_ref[...])
    hn = _rms(h, gffn_ref[...]).astype(BF16)
    d_ff = wfg_ref.shape[1]
    ff = None
    for lo in range(0, d_ff, FF_BLOCK):
        hi = min(lo + FF_BLOCK, d_ff)
        act = _silu(_mm(hn, wfg_ref[:, lo:hi])) * _mm(hn, wfu_ref[:, lo:hi])
        part = _mm(act, wfd_ref[lo:hi, :])
        ff = part if ff is None else ff + part
    h = h + ff
    hp = _rms(h, gple_ref[...]).astype(BF16)
    h = h + _sigmoid(_mm(hp, wpg_ref[...])) * _mm(p_ref[...], wple_ref[...])
    y_ref[...] = _rms(h, gfin_ref[...])


def _mix(x2, oa, ob, p2, g_mix, w, b_gate, w_o, g_ffn, wfg, wfu, wfd, g_ple, w_ple, w_ple_gate,
         g_final, tm, state_args=None):
    n = x2.shape[0]
    steps = n // tm
    in_specs = [
        pl.BlockSpec((tm, D_MODEL), lambda t: (t, 0)),
        pl.BlockSpec((tm, D_MODEL), lambda t: (t, 0)),
        pl.BlockSpec((tm, D_MODEL), lambda t: (t, 0)),
        pl.BlockSpec((tm, D_PLE), lambda t: (t, 0)),
        _full((1, D_MODEL)),
        _w_cols(COL_GATES, 2 * D_MODEL), _full((1, 2 * D_MODEL)), _full(w_o.shape), _full((1, D_MODEL)),
        _full(wfg.shape), _full(wfu.shape), _full(wfd.shape), _full((1, D_MODEL)),
        _full(w_ple.shape), _full(w_ple_gate.shape), _full((1, D_MODEL)),
    ]
    args = [x2, oa, ob, p2, g_mix, w, b_gate, w_o, g_ffn, wfg, wfu, wfd, g_ple, w_ple, w_ple_gate, g_final]
    out_specs = [pl.BlockSpec((tm, D_MODEL), lambda t: (t, 0))]
    out_shape = [jax.ShapeDtypeStruct((n, D_MODEL), F32)]
    if state_args is not None:
        state = state_args[-1]
        assert state_args[1].shape[0] == steps
        row_spec = lambda width: pl.BlockSpec((1, STATE_ROWS, width), lambda t: (t, 0, 0))
        state_spec = pl.BlockSpec((1, STATE_ROWS, H_R, DK_R, DV_R), lambda t: (0, t, 0, 0, 0))
        in_specs = [_smem(), row_spec(QR), row_spec(QR), row_spec(VR), state_spec] + in_specs
        args = list(state_args) + args
        out_specs += [row_spec(VR), state_spec]
        out_shape += [jax.ShapeDtypeStruct((steps, STATE_ROWS, VR), F32),
                      jax.ShapeDtypeStruct(state.shape, F32)]
    return pl.pallas_call(
        functools.partial(_mix_kernel, with_state=state_args is not None),
        grid=(steps,),
        in_specs=in_specs,
        out_specs=out_specs,
        out_shape=out_shape,
        compiler_params=pltpu.CompilerParams(
            dimension_semantics=("arbitrary",), vmem_limit_bytes=VMEM_LIMIT_BYTES),
        name="mix_ffn",
    )(*args)


def _qkv_sample_kernel(x_ref, gmix_ref, wq_ref, wk_ref, wv_ref, cos_ref, sin_ref, q_ref, k_ref, v_ref):
    xn = _rms(x_ref[...], gmix_ref[...]).astype(BF16)
    cos = cos_ref[...]
    sin = sin_ref[...]
    for h in range(H_R):
        hs = slice(h * DK_R, (h + 1) * DK_R)
        q_ref[:, hs] = _rope(_mm(xn, wq_ref[:, hs]), cos, sin)
        k_ref[:, hs] = _rope(_mm(xn, wk_ref[:, hs]), cos, sin) * (DK_R ** -0.5)
    v_ref[...] = _mm(xn, wv_ref[...])


def _qkv_sample(x2, g_mix, w, cos, sin):
    n = x2.shape[0]
    return pl.pallas_call(
        _qkv_sample_kernel,
        grid=(1,),
        in_specs=[_full(x2.shape), _full((1, D_MODEL)), _w_cols(COL_Q, QR), _w_cols(COL_K, QR),
                  _w_cols(COL_V, VR), _full(cos.shape), _full(sin.shape)],
        out_specs=[_full((n, QR)), _full((n, QR)), _full((n, VR))],
        out_shape=[jax.ShapeDtypeStruct((n, QR), F32), jax.ShapeDtypeStruct((n, QR), F32),
                   jax.ShapeDtypeStruct((n, VR), F32)],
        compiler_params=pltpu.CompilerParams(
            dimension_semantics=("arbitrary",), vmem_limit_bytes=VMEM_LIMIT_BYTES),
        name="qkv_sample",
    )(x2, g_mix, w, w, w, cos, sin)


def _branches_sample_kernel(x_ref, o_ref, gmix_ref, wg_ref, wro_ref, wu_ref, wva_ref, lng_ref, lnb_ref,
                            w00_ref, b0_ref, wao_ref, outa_ref, outb_ref, va_ref):
    xn = _rms(x_ref[...], gmix_ref[...]).astype(BF16)
    g = _mm(xn, wg_ref[...])
    o = o_ref[...]
    gated = [_silu(g[:, h * DV_R:(h + 1) * DV_R]) * _group_norm(o[:, h * DV_R:(h + 1) * DV_R])
             for h in range(H_R)]
    outb_ref[...] = _mm(jnp.concatenate(gated, axis=-1), wro_ref[...]).astype(outb_ref.dtype)
    u = _gelu(_mm(xn, wu_ref[...]))
    va = _layer_norm(_gelu(_mm(xn, wva_ref[...])), lng_ref[...], lnb_ref[...])
    va_ref[...] = va
    s = w00_ref[...] * va + b0_ref[...]
    outa_ref[...] = _mm(u * s, wao_ref[...]).astype(outa_ref.dtype)


def _branches_sample(x2, o_raw, g_mix, w, w_ret_out, ln_g, ln_b, w00, b0, w_a_out):
    n = x2.shape[0]
    return pl.pallas_call(
        _branches_sample_kernel,
        grid=(1,),
        in_specs=[_full(x2.shape), _full(o_raw.shape), _full((1, D_MODEL)), _w_cols(COL_G, VR),
                  _full(w_ret_out.shape), _w_cols(COL_U, D_A), _w_cols(COL_VA, D_A), _full((1, D_A)),
                  _full((1, D_A)), _full((1, D_A)), _full((1, D_A)), _full(w_a_out.shape)],
        out_specs=[_full((n, D_MODEL)), _full((n, D_MODEL)), _full((n, D_A))],
        out_shape=[jax.ShapeDtypeStruct((n, D_MODEL), BF16), jax.ShapeDtypeStruct((n, D_MODEL), BF16),
                   jax.ShapeDtypeStruct((n, D_A), F32)],
        compiler_params=pltpu.CompilerParams(
            dimension_semantics=("arbitrary",), vmem_limit_bytes=VMEM_LIMIT_BYTES),
        name="branches_sample",
    )(x2, o_raw, g_mix, w, w_ret_out, w, w, ln_g, ln_b, w00, b0, w_a_out)


def _rope_tables(pos):
    freqs = ROPE_BASE ** (-jnp.arange(HALF, dtype=F32) / HALF)
    ang = pos[:, None] * freqs[None, :]
    return jnp.cos(ang), jnp.sin(ang)


def _decay_tables(length):
    lg = jnp.log1p(-jnp.exp2(-5.0 - jnp.arange(H_R, dtype=F32)))
    idx = jnp.arange(length, dtype=F32)
    diff = idx[:, None] - idx[None, :]
    decay = jnp.where(diff >= 0, jnp.exp(lg[:, None, None] * jnp.maximum(diff, 0.0)), 0.0)
    q_dec = jnp.exp(lg[None, :] * (idx[:, None] + 1.0))
    k_dec = jnp.exp(lg[None, :] * (length - 1.0 - idx[:, None]))
    s_dec = jnp.exp(lg * length)
    return decay, q_dec, k_dec, s_dec


def kernel(x_prompt, x_sample, state_ret, p_prompt, p_sample, g_mix, w_in, w_ret_out, ln_a_g, ln_a_b,
           w_s, b_s, w_a_out, b_gate, w_o, g_ffn, w_ff_gate, w_ff_up, w_ff_down, g_ple, w_ple,
           w_ple_gate, g_final):
    depth = w_in.shape[0]
    assert depth == 1, "single-layer step"
    b, seq, _ = x_prompt.shape
    nb, dec_seq, _ = x_sample.shape
    assert dec_seq == 1 and seq % TOKEN_TILE == 0 and TOKEN_TILE % RET_CHUNK == 0
    assert (b * seq) // MIX_TILE == nb // STATE_ROWS

    i = 0
    w = w_in[i].astype(BF16)
    wro = w_ret_out[i].astype(BF16)
    wao = w_a_out[i].astype(BF16)
    wo = w_o[i].astype(BF16)
    wfg, wfu, wfd = w_ff_gate[i].astype(BF16), w_ff_up[i].astype(BF16), w_ff_down[i].astype(BF16)
    wple, wpg = w_ple[i].astype(BF16), w_ple_gate[i].astype(BF16)
    gmix = g_mix[i][None, :]
    lng, lnb = ln_a_g[i][None, :], ln_a_b[i][None, :]
    bgt = b_gate[i][None, :]
    gffn, gple, gfin = g_ffn[i][None, :], g_ple[i][None, :], g_final[None, :]

    cos_p, sin_p = _rope_tables(jnp.arange(seq, dtype=F32))
    decay, q_dec, k_dec, s_dec = _decay_tables(RET_CHUNK)
    qdec_full = jnp.broadcast_to(q_dec.T[:, :, None], (H_R, RET_CHUNK, DV_R))
    kdec_full = jnp.broadcast_to(k_dec.T[:, :, None], (H_R, RET_CHUNK, DK_R))
    x2 = x_prompt.reshape(b * seq, D_MODEL)
    out_b, st_p = _ret_prompt(x_prompt, gmix, w, cos_p, sin_p, decay, qdec_full, kdec_full,
                              s_dec, wro)
    out_a = _sgu_prompt(x2, gmix, w, lng, lnb, w_s[i], b_s[i][:, :, None], wao)

    cos_s, sin_s = _rope_tables(PAST_LEN + jnp.arange(dec_seq, dtype=F32))
    decay1, q_dec1, k_dec1, s_dec1 = _decay_tables(1)
    dec1 = jnp.stack([decay1[:, 0, 0], q_dec1[0], k_dec1[0], s_dec1], axis=0)
    xs2 = x_sample.reshape(nb, D_MODEL)
    q_s, k_s, v_s = _qkv_sample(xs2, gmix, w, cos_s, sin_s)
    state_args = (dec1, q_s.reshape(nb // STATE_ROWS, STATE_ROWS, QR),
                  k_s.reshape(nb // STATE_ROWS, STATE_ROWS, QR),
                  v_s.reshape(nb // STATE_ROWS, STATE_ROWS, VR), state_ret)
    y_p, o_raw, st_s = _mix(x2, out_a, out_b, p_prompt[i].reshape(b * seq, D_PLE), gmix, w, bgt, wo, gffn,
                            wfg, wfu, wfd, gple, wple, wpg, gfin, MIX_TILE, state_args)

    w00 = jnp.repeat(w_s[i][:, 0, 0], GC_A)[None, :]
    b0 = jnp.repeat(b_s[i][:, 0], GC_A)[None, :]
    oa_s, ob_s, va_s = _branches_sample(xs2, o_raw.reshape(nb, VR), gmix, w, wro, lng, lnb,
                                        w00, b0, wao)
    (y_s,) = _mix(xs2, oa_s, ob_s, p_sample[i].reshape(nb, D_PLE), gmix, w, bgt, wo, gffn, wfg, wfu, wfd,
                  gple, wple, wpg, gfin, nb)

    return (y_p.reshape(b, seq, D_MODEL), y_s.reshape(nb, dec_seq, D_MODEL), st_p, st_s,
            va_s.reshape(1, nb, dec_seq, D_A))
```
